```python
import math, functools
import jax, jax.numpy as jnp
from jax import lax
import numpy as np

D_MODEL = 1024
BATCH = 4
SEQ = 4096
DEPTH = 1
DEC_BATCH = 128
DEC_SEQ = 1
PAST_LEN = 2048
PAGE_SIZE = 128

R_HEADS = 4
R_DK = 128
R_DV = 256
R_CHUNK = 128
R_ROT_BASE = 10000.0
A_HEADS = 4
A_DQ = 64
A_DV = 2 * A_DQ
A_QBLOCK = 128
ROPE_THETA = 500000.0
ROT_DIM = A_DQ // 4
R_QK_W = R_HEADS * R_DK
R_V_W = R_HEADS * R_DV
A_QK_W = A_HEADS * 2 * A_DQ
A_V_W = A_HEADS * A_DV
_IN_SIZES = (R_QK_W, R_QK_W, R_V_W, A_QK_W, A_QK_W, A_V_W, D_MODEL, D_MODEL)
IN_COLS = sum(_IN_SIZES)
IN_SPLITS = tuple(int(s) for s in np.cumsum(_IN_SIZES)[:-1])
P_HEADS = 8
P_NKEYS = 128
P_NEXPERTS = P_NKEYS * P_NKEYS
P_DKEY = 256
P_TOPK = 16
P_TOKBLOCK = 128
NORM_EPS = 1e-6
SUBLN_EPS = 1e-5

kernel_name = 'retention_diffattn_peer_hybrid_step'


def rms_norm(x, g=None, eps=NORM_EPS):
    xf = x.astype(jnp.float32)
    y = xf * lax.rsqrt(jnp.mean(xf * xf, axis=-1, keepdims=True) + eps)
    if g is not None:
        y = y * g.astype(jnp.float32)
    return y.astype(x.dtype)


def retention_rotate(x, posf):
    angle = 1.0 / (R_ROT_BASE ** jnp.linspace(0.0, 1.0, R_DK // 2, dtype=jnp.float32))
    ang = posf[:, None] * angle[None, :]
    cos = jnp.cos(ang)[None, :, None, :].astype(x.dtype)
    sin = jnp.sin(ang)[None, :, None, :].astype(x.dtype)
    xr = x.reshape(x.shape[:-1] + (R_DK // 2, 2))
    x1, x2 = xr[..., 0], xr[..., 1]
    out = jnp.stack([x1 * cos - x2 * sin, x2 * cos + x1 * sin], axis=-1)
    return out.reshape(x.shape)


def partial_rotary(x, posf):
    half = ROT_DIM // 2
    inv = ROPE_THETA ** (-jnp.arange(0, ROT_DIM, 2, dtype=jnp.float32) / ROT_DIM)
    ang = posf[:, None] * inv[None, :]
    cos = jnp.cos(ang)[None, :, None, None, :].astype(x.dtype)
    sin = jnp.sin(ang)[None, :, None, None, :].astype(x.dtype)
    x1 = x[..., :half]
    x2 = x[..., half:ROT_DIM]
    return jnp.concatenate([x1 * cos - x2 * sin, x2 * cos + x1 * sin, x[..., ROT_DIM:]], axis=-1)


def retention_chunk(q, k, v, state, log_decay):
    L = q.shape[1]
    idx = jnp.arange(L, dtype=jnp.float32)
    rel = idx[:, None] - idx[None, :]
    dmask = jnp.where((rel >= 0)[None], jnp.exp(log_decay[:, None, None] * jnp.maximum(rel, 0.0)[None]), 0.0)
    qf, kf, vf = q.astype(jnp.float32), k.astype(jnp.float32), v.astype(jnp.float32)
    scores = jnp.einsum('blhd,bmhd->bhlm', qf, kf) * dmask[None]
    intra = jnp.einsum('bhlm,bmhe->blhe', scores, vf)
    q_decay = jnp.exp(log_decay[None, :] * (idx[:, None] + 1.0))
    inter = jnp.einsum('blhd,bhde->blhe', qf, state) * q_decay[None, :, :, None]
    k_decay = jnp.exp(log_decay[None, :] * (L - 1.0 - idx[:, None]))
    new_state = (state * jnp.exp(log_decay * L)[None, :, None, None]
                 + jnp.einsum('blhd,lh,blhe->bhde', kf, k_decay, vf))
    return intra + inter, new_state


def retention_prompt(q, k, v, log_decay):
    B, S = q.shape[:2]
    n = S // R_CHUNK

    def to_chunks(t):
        return t.reshape((B, n, R_CHUNK) + t.shape[2:]).swapaxes(0, 1)

    def step(state, qkv):
        out, state = retention_chunk(qkv[0], qkv[1], qkv[2], state, log_decay)
        return state, out

    s0 = jnp.zeros((B, R_HEADS, R_DK, R_DV), jnp.float32)
    s_final, outs = lax.scan(step, s0, (to_chunks(q), to_chunks(k), to_chunks(v)))
    return outs.swapaxes(0, 1).reshape(B, S, R_HEADS, R_DV), s_final


def retention_sample(q, k, v, log_decay, state):
    return retention_chunk(q, k, v, state.astype(jnp.float32), log_decay)


def diff_attention(q, k, v, q_pos, k_pos, lam):
    s = jnp.einsum('bqhcd,bkhcd->bhcqk', q, k).astype(jnp.float32) * (A_DQ ** -0.5)
    mask = k_pos[None, :] <= q_pos[:, None]
    s = jnp.where(mask, s, -1e30)
    p = jax.nn.softmax(s, axis=-1)
    a = p[:, :, 0] - lam * p[:, :, 1]
    return jnp.einsum('bhqk,bkhe->bqhe', a.astype(v.dtype), v)


def attention_prompt(q, k, v, lam):
    B, S = q.shape[:2]
    nblk = S // A_QBLOCK
    pos = jnp.arange(S, dtype=jnp.int32)
    qb = q.reshape((B, nblk, A_QBLOCK) + q.shape[2:]).swapaxes(0, 1)
    qpos = pos.reshape(nblk, A_QBLOCK)
    out = lax.map(lambda xs: diff_attention(xs[0], k, v, xs[1], pos, lam), (qb, qpos))
    return out.swapaxes(0, 1).reshape(B, S, A_HEADS, A_DV)


def attention_sample(q, k, v, lam, past_k, past_v, past_len):
    L = q.shape[1]
    k_all = jnp.concatenate([past_k.astype(k.dtype), k], axis=1)
    v_all = jnp.concatenate([past_v.astype(v.dtype), v], axis=1)
    k_pos = jnp.arange(past_len + L, dtype=jnp.int32)
    q_pos = past_len + jnp.arange(L, dtype=jnp.int32)
    return diff_attention(q, k_all, v_all, q_pos, k_pos, lam)


def gather_pages(cache_l, page_table):
    g = jnp.take(cache_l, page_table, axis=0)
    return g.reshape((g.shape[0], g.shape[1] * g.shape[2]) + g.shape[3:])


def peer(z, w_query, sub_keys, expert_u, expert_v):
    T = z.shape[0]
    qry = (z @ w_query).reshape(T, P_HEADS, 2, P_DKEY // 2)
    sc = jnp.einsum('thcd,hcnd->thcn', qry, sub_keys).astype(jnp.float32)
    s_half, i_half = lax.top_k(sc, P_TOPK)
    cand_s = (s_half[:, :, 0, :, None] + s_half[:, :, 1, None, :]).reshape(T, P_HEADS, P_TOPK * P_TOPK)
    cand_i = (i_half[:, :, 0, :, None] * P_NKEYS + i_half[:, :, 1, None, :]).reshape(T, P_HEADS, P_TOPK * P_TOPK)
    top_s, top_pos = lax.top_k(cand_s, P_TOPK)
    ids = jnp.take_along_axis(cand_i, top_pos, axis=-1)
    gate = jax.nn.softmax(top_s, axis=-1)
    u = jnp.take(expert_u, ids, axis=0)
    act = jax.nn.gelu(jnp.einsum('thkd,td->thk', u, z).astype(jnp.float32), approximate=False) * gate
    v = jnp.take(expert_v, ids, axis=0)
    return jnp.einsum('thk,thkd->td', act.astype(z.dtype), v)


def hybrid_layer(x, pos, ret_fn, att_fn, blocked_peer, layer_idx,
                 g_mix, w_in, lam_q1, lam_k1, lam_q2, lam_k2, g_subln,
                 w_br_ret, w_br_att, w_out, g_ffn, w_query, sub_keys, expert_u, expert_v):
    B, L, D = x.shape
    posf = pos.astype(jnp.float32)
    z = rms_norm(x, g_mix)
    rq, rk, rv, aq, ak, av, ga, gb = jnp.split(z @ w_in, IN_SPLITS, axis=-1)
    rq = retention_rotate(rq.reshape(B, L, R_HEADS, R_DK), posf)
    rk = retention_rotate(rk.reshape(B, L, R_HEADS, R_DK), posf) * (R_DK ** -0.5)
    rv = rv.reshape(B, L, R_HEADS, R_DV)
    aq = partial_rotary(aq.reshape(B, L, A_HEADS, 2, A_DQ), posf)
    ak = partial_rotary(ak.reshape(B, L, A_HEADS, 2, A_DQ), posf)
    av = av.reshape(B, L, A_HEADS, A_DV)
    log_decay = jnp.log(1.0 - 2.0 ** (-5.0 - jnp.arange(R_HEADS, dtype=jnp.float32)))
    r_out, r_state = ret_fn(rq, rk, rv, log_decay)
    r_out = rms_norm(r_out).astype(x.dtype).reshape(B, L, R_V_W)
    lam_init = 0.8 - 0.6 * math.exp(-0.3 * layer_idx)
    lam = (jnp.exp(jnp.sum(lam_q1.astype(jnp.float32) * lam_k1.astype(jnp.float32)))
           - jnp.exp(jnp.sum(lam_q2.astype(jnp.float32) * lam_k2.astype(jnp.float32))) + lam_init)
    a_out = att_fn(aq, ak, av, lam)
    a_out = (rms_norm(a_out, g_subln, SUBLN_EPS) * (1.0 - lam_init)).reshape(B, L, A_V_W)
    m = jax.nn.sigmoid(ga) * (r_out @ w_br_ret) + jax.nn.sigmoid(gb) * (a_out @ w_br_att)
    h = x + m @ w_out
    z2 = rms_norm(h, g_ffn).reshape(B * L, D)
    if blocked_peer:
        f = lax.map(lambda zz: peer(zz, w_query, sub_keys, expert_u, expert_v),
                    z2.reshape(-1, P_TOKBLOCK, D))
    else:
        f = peer(z2, w_query, sub_keys, expert_u, expert_v)
    h = h + f.reshape(B, L, D)
    return h, r_state, ak.reshape(B, L, A_HEADS, 2 * A_DQ), av


def setup_inputs(seed: int = 0) -> dict:
    key = jax.random.key(seed)
    ks = jax.random.split(key, 24)
    f32 = jnp.float32
    n_pages = PAST_LEN // PAGE_SIZE
    n_pool = (DEC_BATCH * n_pages * 5) // 4

    def nrm(k, shape, scale):
        return jax.random.normal(k, shape, f32) * scale

    def gain(k, shape):
        return 1.0 + 0.01 * jax.random.normal(k, shape, f32)

    page_table = jax.random.permutation(ks[5], n_pool)[:DEC_BATCH * n_pages].reshape(DEC_BATCH, n_pages).astype(jnp.int32)
    return {
        'x_prompt': nrm(ks[0], (BATCH, SEQ, D_MODEL), 1.0),
        'x_sample': nrm(ks[1], (DEC_BATCH, DEC_SEQ, D_MODEL), 1.0),
        'state_ret': nrm(ks[2], (DEPTH, DEC_BATCH, R_HEADS, R_DK, R_DV), 0.5),
        'cache_k': nrm(ks[3], (DEPTH, n_pool, PAGE_SIZE, A_HEADS, 2 * A_DQ), 1.0),
        'cache_v': nrm(ks[4], (DEPTH, n_pool, PAGE_SIZE, A_HEADS, A_DV), 1.0),
        'page_table': page_table,
        'g_mix': gain(ks[6], (DEPTH, D_MODEL)),
        'w_in': nrm(ks[7], (DEPTH, D_MODEL, IN_COLS), D_MODEL ** -0.5),
        'lam_q1': nrm(ks[8], (DEPTH, A_DQ), 0.1),
        'lam_k1': nrm(ks[9], (DEPTH, A_DQ), 0.1),
        'lam_q2': nrm(ks[10], (DEPTH, A_DQ), 0.1),
        'lam_k2': nrm(ks[11], (DEPTH, A_DQ), 0.1),
        'g_subln': gain(ks[12], (DEPTH, A_DV)),
        'w_br_ret': nrm(ks[13], (DEPTH, R_V_W, D_MODEL), R_V_W ** -0.5),
        'w_br_att': nrm(ks[14], (DEPTH, A_V_W, D_MODEL), A_V_W ** -0.5),
        'w_out': nrm(ks[15], (DEPTH, D_MODEL, D_MODEL), D_MODEL ** -0.5),
        'g_ffn': gain(ks[16], (DEPTH, D_MODEL)),
        'w_query': nrm(ks[17], (DEPTH, D_MODEL, P_HEADS * P_DKEY), D_MODEL ** -0.5),
        'sub_keys': nrm(ks[18], (DEPTH, P_HEADS, 2, P_NKEYS, P_DKEY // 2), (P_DKEY // 2) ** -0.5),
        'expert_u': nrm(ks[19], (DEPTH, P_NEXPERTS, D_MODEL), D_MODEL ** -0.5),
        'expert_v': nrm(ks[20], (DEPTH, P_NEXPERTS, D_MODEL), 0.5),
        'g_final': gain(ks[21], (D_MODEL,)),
    }


def reference(x_prompt, x_sample, state_ret, cache_k, cache_v, page_table, g_mix, w_in,
              lam_q1, lam_k1, lam_q2, lam_k2, g_subln, w_br_ret, w_br_att, w_out, g_ffn,
              w_query, sub_keys, expert_u, expert_v, g_final):
    S = x_prompt.shape[1]
    DB, L = x_sample.shape[0], x_sample.shape[1]
    past_len = page_table.shape[1] * cache_k.shape[2]
    pos_p = jnp.arange(S, dtype=jnp.int32)
    pos_s = past_len + jnp.arange(L, dtype=jnp.int32)
    hp, hs = x_prompt, x_sample
    rs_p, k_p, v_p, rs_s, k_s, v_s = [], [], [], [], [], []
    for l in range(DEPTH):
        w = (g_mix[l], w_in[l], lam_q1[l], lam_k1[l], lam_q2[l], lam_k2[l], g_subln[l],
             w_br_ret[l], w_br_att[l], w_out[l], g_ffn[l], w_query[l], sub_keys[l], expert_u[l], expert_v[l])
        hp, st, kr, vr = hybrid_layer(hp, pos_p, retention_prompt, attention_prompt, True, l, *w)
        rs_p.append(st)
        k_p.append(kr)
        v_p.append(vr)
        past_k = gather_pages(cache_k[l], page_table).reshape(DB, past_len, A_HEADS, 2, A_DQ)
        past_v = gather_pages(cache_v[l], page_table)
        ret_fn = functools.partial(retention_sample, state=state_ret[l])
        att_fn = functools.partial(attention_sample, past_k=past_k, past_v=past_v, past_len=past_len)
        hs, st, kr, vr = hybrid_layer(hs, pos_s, ret_fn, att_fn, False, l, *w)
        rs_s.append(st)
        k_s.append(kr)
        v_s.append(vr)
    y_prompt = rms_norm(hp, g_final)
    y_sample = rms_norm(hs, g_final)
    return (y_prompt, y_sample, jnp.stack(rs_p), jnp.stack(k_p), jnp.stack(v_p),
            jnp.stack(rs_s), jnp.stack(k_s), jnp.stack(v_s))
```

```python
import functools
import math

import jax
import jax.numpy as jnp
import numpy as np
from jax import lax
from jax.experimental import pallas as pl
from jax.experimental.pallas import tpu as pltpu

F32 = jnp.float32
BF16 = jnp.bfloat16

D_MODEL = 1024
R_HEADS, R_DK, R_DV = 4, 128, 256
R_ROT_BASE = 10000.0
A_HEADS, A_DQ, A_DV = 4, 64, 128
ROPE_THETA = 500000.0
ROT_DIM = A_DQ // 4
R_QK_W = R_HEADS * R_DK
R_V_W = R_HEADS * R_DV
A_QK_W = A_HEADS * 2 * A_DQ
A_V_W = A_HEADS * A_DV
IN_SIZES = (R_QK_W, R_QK_W, R_V_W, A_QK_W, A_QK_W, A_V_W, D_MODEL, D_MODEL)
IN_OFFS = tuple(int(v) for v in np.cumsum((0,) + IN_SIZES))
NORM_EPS = 1e-6
SUBLN_EPS = 1e-5
LANES = 128
V7X_VMEM_LIMIT = 56 * 1024 * 1024


def _cparams(sem):
    return pltpu.CompilerParams(dimension_semantics=sem, vmem_limit_bytes=V7X_VMEM_LIMIT)


def _const_spec(shape):
    nd = len(shape)
    return pl.BlockSpec(shape, lambda *_: (0,) * nd)


def _rotation_tables(posf):
    lane = jnp.arange(LANES)
    angle = 1.0 / (R_ROT_BASE ** jnp.linspace(0.0, 1.0, R_DK // 2, dtype=F32))
    ang = posf[:, None] * angle[None, :]
    cos_r = jnp.repeat(jnp.cos(ang), 2, axis=1)
    sin_r = jnp.repeat(jnp.sin(ang), 2, axis=1)
    even = (lane % 2 == 0)[None, :]
    sr_even = jnp.where(even, -sin_r, 0.0)
    sr_odd = jnp.where(even, 0.0, sin_r)
    half = ROT_DIM // 2
    inv = ROPE_THETA ** (-jnp.arange(0, ROT_DIM, 2, dtype=F32) / ROT_DIM)
    ang_a = posf[:, None] * inv[None, :]
    within = lane % A_DQ
    idx = within % half
    cos_a = jnp.where((within < ROT_DIM)[None, :], jnp.cos(ang_a)[:, idx], 1.0)
    sin_a = jnp.sin(ang_a)[:, idx]
    sa_lo = jnp.where((within < half)[None, :], -sin_a, 0.0)
    sa_hi = jnp.where(((within >= half) & (within < ROT_DIM))[None, :], sin_a, 0.0)
    return tuple(t.astype(F32) for t in (cos_r, sr_even, sr_odd, cos_a, sa_lo, sa_hi))


def _inproj_kernel(x_ref, g_ref, w_ref, cr_ref, se_ref, so_ref, ca_ref, sl_ref, sh_ref,
                   rq_ref, rk_ref, rv_ref, aq_ref, akb_ref, akf_ref, avb_ref, avf_ref, ga_ref, gb_ref):
    x = x_ref[...]
    ms = jnp.mean(x * x, axis=-1, keepdims=True)
    z = (x * lax.rsqrt(ms + NORM_EPS) * g_ref[...]).astype(BF16)

    def proj(k):
        return jnp.dot(z, w_ref[:, IN_OFFS[k]:IN_OFFS[k + 1]], preferred_element_type=F32)

    cr, se, so = cr_ref[...], se_ref[...], so_ref[...]
    ca, sl, sh = ca_ref[...], sl_ref[...], sh_ref[...]

    def ret_rot(t):
        return t * cr + pltpu.roll(t, LANES - 1, 1) * se + pltpu.roll(t, 1, 1) * so

    def att_rot(t):
        h = ROT_DIM // 2
        return t * ca + pltpu.roll(t, LANES - h, 1) * sl + pltpu.roll(t, h, 1) * sh

    rq = proj(0)
    rk = proj(1)
    for hh in range(R_HEADS):
        s = slice(hh * LANES, (hh + 1) * LANES)
        rq_ref[:, s] = ret_rot(rq[:, s]).astype(BF16)
        rk_ref[:, s] = (ret_rot(rk[:, s]) * (R_DK ** -0.5)).astype(BF16)
    rv_ref[...] = proj(2).astype(BF16)
    aq = proj(3)
    ak = proj(4)
    for hh in range(A_HEADS):
        s = slice(hh * LANES, (hh + 1) * LANES)
        aq_ref[:, s] = (att_rot(aq[:, s]) * (A_DQ ** -0.5)).astype(BF16)
        akr = att_rot(ak[:, s])
        akf_ref[:, s] = akr
        akb_ref[:, s] = akr.astype(BF16)
    av = proj(5)
    avf_ref[...] = av
    avb_ref[...] = av.astype(BF16)
    ga_ref[...] = proj(6)
    gb_ref[...] = proj(7)


def _inproj(x2d, g_mix, w_in_bf, tables, tm):
    T = x2d.shape[0]
    P = tables[0].shape[0]
    nper = P // tm
    row = lambda i: (i, 0)
    tab = pl.BlockSpec((tm, LANES), lambda i: (i % nper, 0))
    widths_dtypes = ((R_QK_W, BF16), (R_QK_W, BF16), (R_V_W, BF16), (A_QK_W, BF16), (A_QK_W, BF16), (A_QK_W, F32),
                     (A_V_W, BF16), (A_V_W, F32), (D_MODEL, F32), (D_MODEL, F32))
    return pl.pallas_call(
        _inproj_kernel,
        grid=(T // tm,),
        in_specs=[pl.BlockSpec((tm, D_MODEL), row), _const_spec((1, D_MODEL)), _const_spec(w_in_bf.shape)] + [tab] * 6,
        out_specs=[pl.BlockSpec((tm, w), row) for w, _ in widths_dtypes],
        out_shape=[jax.ShapeDtypeStruct((T, w), dt) for w, dt in widths_dtypes],
        compiler_params=_cparams(("parallel",)),
        name="inproj",
    )(x2d, g_mix.reshape(1, D_MODEL), w_in_bf, *tables)


def _log_decay(h):
    return float(np.log(np.float32(1.0) - np.float32(2.0) ** np.float32(-5.0 - h)))


def _rms(o, eps):
    return o * lax.rsqrt(jnp.mean(o * o, axis=-1, keepdims=True) + eps)


def _ret_prompt_kernel(q_ref, k_ref, v_ref, o_ref, st_ref):
    C = q_ref.shape[0]

    @pl.when(pl.program_id(1) == 0)
    def _():
        st_ref[...] = jnp.zeros_like(st_ref)

    li = lax.broadcasted_iota(jnp.int32, (C, C), 0)
    mi = lax.broadcasted_iota(jnp.int32, (C, C), 1)
    rel = (li - mi).astype(F32)
    row = lax.broadcasted_iota(jnp.int32, (C, 1), 0).astype(F32)
    for h in range(R_HEADS):
        lg = _log_decay(h)
        dmask = jnp.where(rel >= 0.0, jnp.exp(lg * jnp.maximum(rel, 0.0)), 0.0)
        q = q_ref[:, h * R_DK:(h + 1) * R_DK]
        k = k_ref[:, h * R_DK:(h + 1) * R_DK]
        v = v_ref[:, h * R_DV:(h + 1) * R_DV]
        st = st_ref[0, h]
        s = lax.dot_general(q, k, (((1,), (1,)), ((), ())), preferred_element_type=F32)
        intra = jnp.dot((s * dmask).astype(BF16), v, preferred_element_type=F32)
        inter = jnp.dot(q, st.astype(BF16), preferred_element_type=F32) * jnp.exp(lg * (row + 1.0))
        o_ref[:, h * R_DV:(h + 1) * R_DV] = _rms(intra + inter, NORM_EPS).astype(BF16)
        kd = (k.astype(F32) * jnp.exp(lg * (C - 1.0 - row))).astype(BF16)
        upd = lax.dot_general(kd, v, (((0,), (0,)), ((), ())), preferred_element_type=F32)
        st_ref[0, h] = st * math.exp(lg * C) + upd


def _ret_prompt(rq, rk, rv, B, S, C):
    n = S // C
    row = lambda b, c: (b * n + c, 0)
    return pl.pallas_call(
        _ret_prompt_kernel,
        grid=(B, n),
        in_specs=[pl.BlockSpec((C, R_QK_W), row), pl.BlockSpec((C, R_QK_W), row), pl.BlockSpec((C, R_V_W), row)],
        out_specs=[pl.BlockSpec((C, R_V_W), row), pl.BlockSpec((1, R_HEADS, R_DK, R_DV), lambda b, c: (b, 0, 0, 0))],
        out_shape=[jax.ShapeDtypeStruct((B * S, R_V_W), BF16), jax.ShapeDtypeStruct((B, R_HEADS, R_DK, R_DV), F32)],
        compiler_params=_cparams(("parallel", "arbitrary")),
        name="ret_prompt",
    )(rq, rk, rv)


def _lambda(lam_ref, lam_init):
    lam = lam_ref[...]
    l1 = jnp.sum(lam[0:1] * lam[1:2], axis=-1, keepdims=True)
    l2 = jnp.sum(lam[2:3] * lam[3:4], axis=-1, keepdims=True)
    return jnp.exp(l1) - jnp.exp(l2) + lam_init


def _attn_prompt_kernel(lam_ref, gs_ref, q_ref, k_ref, v_ref, o_ref, q2_ref, m_ref, l_ref, acc_ref, *, lam_init):
    qi = pl.program_id(2)
    ki = pl.program_id(3)
    tq = q_ref.shape[0]
    tk = k_ref.shape[0]

    @pl.when(ki == 0)
    def _():
        q = q_ref[...]
        lane = lax.broadcasted_iota(jnp.int32, q.shape, 1)
        zero = jnp.zeros_like(q)
        q2_ref[0:tq] = jnp.where(lane < A_DQ, q, zero)
        q2_ref[tq:2 * tq] = jnp.where(lane >= A_DQ, q, zero)
        m_ref[...] = jnp.full_like(m_ref, -jnp.inf)
        l_ref[...] = jnp.zeros_like(l_ref)
        acc_ref[...] = jnp.zeros_like(acc_ref)

    def step(masked):
        s = lax.dot_general(q2_ref[...], k_ref[...], (((1,), (1,)), ((), ())), preferred_element_type=F32)
        if masked:
            r = lax.broadcasted_iota(jnp.int32, (tq, tk), 0)
            c = lax.broadcasted_iota(jnp.int32, (tq, tk), 1)
            keep = c <= r
            keep = jnp.concatenate([keep, keep], axis=0)
            s = jnp.where(keep, s, -1e30)
        m_prev = m_ref[...]
        m_new = jnp.maximum(m_prev, jnp.max(s, axis=-1, keepdims=True))
        alpha = jnp.exp(m_prev - m_new)
        p = jnp.exp(s - pltpu.repeat(m_new, tk // LANES, axis=1))
        l_ref[...] = alpha * l_ref[...] + jnp.sum(p, axis=-1, keepdims=True)
        acc_ref[...] = alpha * acc_ref[...] + jnp.dot(p.astype(BF16), v_ref[...], preferred_element_type=F32)
        m_ref[...] = m_new

    @pl.when(ki < qi)
    def _():
        step(False)

    @pl.when(ki == qi)
    def _():
        step(True)
        lam = _lambda(lam_ref, lam_init)
        o = acc_ref[0:tq] / l_ref[0:tq] - lam * (acc_ref[tq:2 * tq] / l_ref[tq:2 * tq])
        o_ref[...] = (_rms(o, SUBLN_EPS) * gs_ref[...] * (1.0 - lam_init)).astype(BF16)


def _attn_prompt(aq, ak, av, lam4, g_subln, lam_init, B, S, t):
    n = S // t
    return pl.pallas_call(
        functools.partial(_attn_prompt_kernel, lam_init=lam_init),
        grid=(B, A_HEADS, n, n),
        in_specs=[_const_spec((4, A_DQ)), _const_spec((1, A_DV)),
                  pl.BlockSpec((t, LANES), lambda b, h, qi, ki: (b * n + qi, h)),
                  pl.BlockSpec((t, LANES), lambda b, h, qi, ki: (b * n + jnp.minimum(ki, qi), h)),
                  pl.BlockSpec((t, LANES), lambda b, h, qi, ki: (b * n + jnp.minimum(ki, qi), h))],
        out_specs=pl.BlockSpec((t, LANES), lambda b, h, qi, ki: (b * n + qi, h)),
        out_shape=jax.ShapeDtypeStruct((B * S, A_V_W), BF16),
        scratch_shapes=[pltpu.VMEM((2 * t, LANES), BF16), pltpu.VMEM((2 * t, LANES), F32),
                        pltpu.VMEM((2 * t, LANES), F32), pltpu.VMEM((2 * t, LANES), F32)],
        compiler_params=_cparams(("parallel", "parallel", "parallel", "arbitrary")),
        name="attn_prompt",
    )(lam4, g_subln.reshape(1, A_DV), aq, ak, av)


P_HEADS, P_NKEYS, P_DKEY, P_TOPK = 8, 128, 256, 16
P_HALF = P_DKEY // 2
P_QW = P_HEADS * P_DKEY


def _merge_kernel(x_ref, r_ref, a_ref, ga_ref, gb_ref, wr_ref, wa_ref, wo_ref, gf_ref, wq_ref, sk_ref,
                  h_ref, z2_ref, sc_ref):
    mr = jnp.dot(r_ref[...], wr_ref[...], preferred_element_type=F32)
    ma = jnp.dot(a_ref[...], wa_ref[...], preferred_element_type=F32)
    m = jax.nn.sigmoid(ga_ref[...]) * mr + jax.nn.sigmoid(gb_ref[...]) * ma
    h = x_ref[...] + jnp.dot(m.astype(BF16), wo_ref[...], preferred_element_type=F32)
    h_ref[...] = h
    z2 = (_rms(h, NORM_EPS) * gf_ref[...]).astype(BF16)
    z2_ref[...] = z2
    qry = jnp.dot(z2, wq_ref[...], preferred_element_type=F32).astype(BF16)
    for hc in range(2 * P_HEADS):
        s = slice(hc * P_HALF, (hc + 1) * P_HALF)
        sc_ref[hc * P_NKEYS:(hc + 1) * P_NKEYS, :] = lax.dot_general(
            sk_ref[hc], qry[:, s], (((1,), (1,)), ((), ())), preferred_element_type=F32)


def _merge(x2d, r_out, a_out, ga, gb, wr, wa, wo, g_ffn, wq, sk, tm):
    T = x2d.shape[0]
    row = lambda i: (i, 0)
    nsc = 2 * P_HEADS * P_NKEYS
    return pl.pallas_call(
        _merge_kernel,
        grid=(T // tm,),
        in_specs=[pl.BlockSpec((tm, D_MODEL), row), pl.BlockSpec((tm, R_V_W), row), pl.BlockSpec((tm, A_V_W), row),
                  pl.BlockSpec((tm, D_MODEL), row), pl.BlockSpec((tm, D_MODEL), row),
                  _const_spec(wr.shape), _const_spec(wa.shape), _const_spec(wo.shape), _const_spec((1, D_MODEL)),
                  _const_spec(wq.shape), _const_spec(sk.shape)],
        out_specs=[pl.BlockSpec((tm, D_MODEL), row), pl.BlockSpec((tm, D_MODEL), row),
                   pl.BlockSpec((nsc, tm), lambda i: (0, i))],
        out_shape=[jax.ShapeDtypeStruct((T, D_MODEL), F32), jax.ShapeDtypeStruct((T, D_MODEL), BF16),
                   jax.ShapeDtypeStruct((nsc, T), F32)],
        compiler_params=_cparams(("parallel",)),
        name="merge",
    )(x2d, r_out, a_out, ga, gb, wr, wa, wo, g_ffn.reshape(1, D_MODEL), wq, sk)


_CAND_LIMITS = tuple(P_TOPK // (r + 1) for r in range(P_TOPK))
_CAND_ROWS = 80


def _extract_top(vals, order, rounds):
    cur = vals
    taken = jnp.full(vals.shape, float(rounds), F32)
    big = float(2 ** 20)
    tops = []
    for r in range(rounds):
        m = jnp.max(cur, axis=0, keepdims=True)
        first = jnp.min(jnp.where(cur == m, order, big), axis=0, keepdims=True)
        sel = order == first
        cur = jnp.where(sel, -jnp.inf, cur)
        taken = jnp.where(sel, float(r), taken)
        tops.append(m)
    return tops, taken


def _route_kernel(sc_ref, n_ref, a_ref, k_ref, b_ref):
    t = sc_ref.shape[1]
    key_order = lax.broadcasted_iota(jnp.int32, (P_NKEYS, t), 0).astype(F32)
    crow = lax.broadcasted_iota(jnp.int32, (_CAND_ROWS, t), 0)
    cand_r = jnp.where(crow < 16, 0, jnp.where(crow < 72, (crow - 8) // 8, crow - 64))
    cand_c = jnp.where(crow < 16, crow, jnp.where(crow < 72, crow % 8, 0))
    limit = jnp.zeros_like(crow)
    for r in range(P_TOPK):
        limit = jnp.where(cand_r == r, _CAND_LIMITS[r], limit)
    cand_valid = cand_c < limit
    cand_pos = (cand_r * P_TOPK + cand_c).astype(F32)

    def head(h, carry):
        s0 = sc_ref[pl.ds(pl.multiple_of(h * 2 * P_NKEYS, P_NKEYS), P_NKEYS), :]
        s1 = sc_ref[pl.ds(pl.multiple_of(h * 2 * P_NKEYS + P_NKEYS, P_NKEYS), P_NKEYS), :]
        top0, rank0 = _extract_top(s0, key_order, P_TOPK)
        top1, rank1 = _extract_top(s1, key_order, P_TOPK)
        s1_lo = jnp.concatenate(top1[0:8], axis=0)
        s1_hi = jnp.concatenate(top1[8:16], axis=0)
        s0_hi = jnp.concatenate(top0[8:16], axis=0)
        cand = jnp.concatenate([top0[0] + s1_lo, top0[0] + s1_hi] + [top0[r] + s1_lo for r in range(1, 8)]
                               + [s0_hi + top1[0]], axis=0)
        cand = jnp.where(cand_valid, cand, -jnp.inf)
        ctop, ctaken = _extract_top(cand, cand_pos, P_TOPK)
        chosen = ctaken < float(P_TOPK)
        e = jnp.where(chosen, jnp.exp(cand - ctop[0]), 0.0)
        z_inv = 1.0 / jnp.sum(e, axis=0, keepdims=True)
        cnt = chosen.astype(F32)
        n_rows = [jnp.sum(cnt[0:16], axis=0, keepdims=True)]
        n_rows += [jnp.sum(cnt[16 + 8 * (r - 1):24 + 8 * (r - 1)], axis=0, keepdims=True) for r in range(1, 8)]
        n_rows += [cnt[72 + r:73 + r] for r in range(8)]
        n_dense = jnp.zeros((P_NKEYS, t), F32)
        for r in range(P_TOPK):
            n_dense = jnp.where(rank0 == float(r), n_rows[r], n_dense)
        o = pl.ds(pl.multiple_of(h * P_NKEYS, P_NKEYS), P_NKEYS)
        n_ref[o, :] = n_dense
        a_ref[o, :] = jnp.exp(s0 - top0[0]) * z_inv
        k_ref[o, :] = rank1
        b_ref[o, :] = jnp.exp(s1 - top1[0])
        return carry

    lax.fori_loop(0, P_HEADS, head, 0)


def _route(sc_t, tg):
    T = sc_t.shape[1]
    col = lambda i: (0, i)
    out = jax.ShapeDtypeStruct((P_HEADS * P_NKEYS, T), F32)
    return pl.pallas_call(
        _route_kernel,
        grid=(T // tg,),
        in_specs=[pl.BlockSpec((2 * P_HEADS * P_NKEYS, tg), col)],
        out_specs=[pl.BlockSpec((P_HEADS * P_NKEYS, tg), col)] * 4,
        out_shape=[out] * 4,
        compiler_params=_cparams(("parallel",)),
        name="route",
    )(sc_t)


def _expert_kernel(z_ref, h_ref, u_ref, vt_ref, n_ref, a_ref, k_ref, b_ref, gfin_ref, y_ref, acc_ref, act_ref, *,
                   final_norm):
    e = pl.program_id(1)
    te = u_ref.shape[0]
    per = te // P_NKEYS

    @pl.when(e == 0)
    def _():
        acc_ref[...] = jnp.zeros_like(acc_ref)

    z = z_ref[...]
    for ii in range(per):
        i = e * per + ii
        hid = lax.dot_general(u_ref[ii * P_NKEYS:(ii + 1) * P_NKEYS, :], z, (((1,), (1,)), ((), ())),
                              preferred_element_type=F32)
        gate = jnp.zeros(hid.shape, F32)
        for h in range(P_HEADS):
            n_row = n_ref[pl.ds(h * P_NKEYS + i, 1), :]
            a_row = a_ref[pl.ds(h * P_NKEYS + i, 1), :]
            kj = k_ref[h * P_NKEYS:(h + 1) * P_NKEYS, :]
            bj = b_ref[h * P_NKEYS:(h + 1) * P_NKEYS, :]
            gate = gate + a_row * jnp.where(kj < n_row, bj, 0.0)
        act = 0.5 * hid * (1.0 + lax.erf(hid * math.sqrt(0.5))) * gate
        act_ref[ii * P_NKEYS:(ii + 1) * P_NKEYS, :] = act.astype(BF16)
    acc_ref[...] += jnp.dot(vt_ref[...], act_ref[...], preferred_element_type=F32)

    @pl.when(e == pl.num_programs(1) - 1)
    def _():
        hf = h_ref[...] + acc_ref[...].T
        y_ref[...] = _rms(hf, NORM_EPS) * gfin_ref[...] if final_norm else hf


def _experts(z2, h, u_bf, vt_bf, rn, ra, rk, rb, g_final, final_norm, tm, te):
    T = z2.shape[0]
    E = u_bf.shape[0]
    row = lambda i, e: (i, 0)
    rcol = lambda i, e: (0, i)
    rspec = pl.BlockSpec((P_HEADS * P_NKEYS, tm), rcol)
    return pl.pallas_call(
        functools.partial(_expert_kernel, final_norm=final_norm),
        grid=(T // tm, E // te),
        in_specs=[pl.BlockSpec((tm, D_MODEL), row), pl.BlockSpec((tm, D_MODEL), row),
                  pl.BlockSpec((te, D_MODEL), lambda i, e: (e, 0)), pl.BlockSpec((D_MODEL, te), lambda i, e: (0, e)),
                  rspec, rspec, rspec, rspec, _const_spec((1, D_MODEL))],
        out_specs=pl.BlockSpec((tm, D_MODEL), row),
        out_shape=jax.ShapeDtypeStruct((T, D_MODEL), F32),
        scratch_shapes=[pltpu.VMEM((D_MODEL, tm), F32), pltpu.VMEM((te, tm), BF16)],
        compiler_params=_cparams(("parallel", "arbitrary")),
        name="experts",
    )(z2, h, u_bf, vt_bf, rn, ra, rk, rb, g_final.reshape(1, D_MODEL))


def _ret_sample_kernel(q_ref, k_ref, v_ref, st_ref, o_ref, ns_ref):
    nb = q_ref.shape[0]
    q = q_ref[...].astype(F32)
    k = k_ref[...].astype(F32)
    v = v_ref[...].astype(F32)
    for h in range(R_HEADS):
        gamma = math.exp(_log_decay(h))
        qh = q[:, h * R_DK:(h + 1) * R_DK]
        kh = k[:, h * R_DK:(h + 1) * R_DK]
        vh = v[:, h * R_DV:(h + 1) * R_DV]
        q_t = qh.T
        k_t = kh.T
        qk = jnp.sum(qh * kh, axis=-1, keepdims=True)
        rows = []
        for b in range(nb):
            st = st_ref[b, h]
            vrow = vh[b:b + 1]
            inter = jnp.sum(q_t[:, b:b + 1] * st, axis=0, keepdims=True) * gamma
            rows.append(qk[b:b + 1] * vrow + inter)
            ns_ref[b, h] = st * gamma + k_t[:, b:b + 1] * vrow
        o = jnp.concatenate(rows, axis=0)
        o_ref[:, h * R_DV:(h + 1) * R_DV] = _rms(o, NORM_EPS).astype(BF16)


def _ret_sample(rq, rk, rv, state, nb):
    DB = rq.shape[0]
    row = lambda i: (i, 0)
    st_spec = pl.BlockSpec((nb, R_HEADS, R_DK, R_DV), lambda i: (i, 0, 0, 0))
    return pl.pallas_call(
        _ret_sample_kernel,
        grid=(DB // nb,),
        in_specs=[pl.BlockSpec((nb, R_QK_W), row), pl.BlockSpec((nb, R_QK_W), row), pl.BlockSpec((nb, R_V_W), row), st_spec],
        out_specs=[pl.BlockSpec((nb, R_V_W), row), st_spec],
        out_shape=[jax.ShapeDtypeStruct((DB, R_V_W), BF16), jax.ShapeDtypeStruct(state.shape, F32)],
        compiler_params=_cparams(("parallel",)),
        name="ret_sample",
    )(rq, rk, rv, state)


def _attn_sample_kernel(pt_ref, lam_ref, gs_ref, q_ref, kn_ref, vn_ref, *rest, npages, lam_init):
    kp = rest[:npages]
    vp = rest[npages:2 * npages]
    o_ref = rest[2 * npages]
    nmap = 2 * A_HEADS
    q = q_ref[0]
    sub = lax.broadcasted_iota(jnp.int32, (nmap, A_QK_W), 0)
    lane = lax.broadcasted_iota(jnp.int32, (nmap, A_QK_W), 1)
    qblk = jnp.where(lane // A_DQ == sub, q.astype(F32), 0.0).astype(BF16)
    nt = (((1,), (1,)), ((), ()))
    scores = [lax.dot_general(kp[p][0].astype(BF16), qblk, nt, preferred_element_type=F32) for p in range(npages)]
    k_new = jnp.broadcast_to(kn_ref[0], (nmap, A_QK_W)).astype(BF16)
    s_new = lax.dot_general(k_new, qblk, nt, preferred_element_type=F32)[0:1]
    m = s_new
    for s in scores:
        m = jnp.maximum(m, jnp.max(s, axis=0, keepdims=True))
    p_new = jnp.exp(s_new - m)
    probs = [jnp.exp(s - m) for s in scores]
    l = p_new
    for p in probs:
        l = l + jnp.sum(p, axis=0, keepdims=True)
    inv_l = 1.0 / l
    tn = (((0,), (0,)), ((), ()))
    acc = jnp.zeros((nmap, A_V_W), F32)
    for p in range(npages):
        acc = acc + lax.dot_general((probs[p] * inv_l).astype(BF16), vp[p][0].astype(BF16), tn,
                                    preferred_element_type=F32)
    pn_new = p_new * inv_l
    lam = _lambda(lam_ref, lam_init)
    v_new = vn_ref[0]
    outs = []
    for h in range(A_HEADS):
        s = slice(h * A_DV, (h + 1) * A_DV)
        w_new = pn_new[:, 2 * h:2 * h + 1] - lam * pn_new[:, 2 * h + 1:2 * h + 2]
        o = acc[2 * h:2 * h + 1, s] - lam * acc[2 * h + 1:2 * h + 2, s] + w_new * v_new[:, s]
        outs.append(_rms(o, SUBLN_EPS) * gs_ref[...] * (1.0 - lam_init))
    o_ref[0] = jnp.concatenate(outs, axis=1).astype(BF16)


def _attn_sample(aq, akf, avf, cache_k, cache_v, page_table, lam4, g_subln, lam_init):
    DB = aq.shape[0]
    npages = page_table.shape[1]
    ck = cache_k.reshape(cache_k.shape[0], cache_k.shape[1], A_QK_W)
    cv = cache_v.reshape(cache_v.shape[0], cache_v.shape[1], A_V_W)
    page = ck.shape[1]
    tok = pl.BlockSpec((1, 1, A_QK_W), lambda b, pt: (b, 0, 0))
    page_specs = [pl.BlockSpec((1, page, A_QK_W), functools.partial(lambda b, pt, p: (pt[b, p], 0, 0), p=p))
                  for p in range(npages)]
    grid_spec = pltpu.PrefetchScalarGridSpec(
        num_scalar_prefetch=1, grid=(DB,),
        in_specs=[pl.BlockSpec((4, A_DQ), lambda b, pt: (0, 0)), pl.BlockSpec((1, A_DV), lambda b, pt: (0, 0)),
                  tok, tok, tok] + page_specs + page_specs,
        out_specs=tok)
    out = pl.pallas_call(
        functools.partial(_attn_sample_kernel, npages=npages, lam_init=lam_init),
        grid_spec=grid_spec,
        out_shape=jax.ShapeDtypeStruct((DB, 1, A_V_W), BF16),
        compiler_params=_cparams(("parallel",)),
        name="attn_sample",
    )(page_table, lam4, g_subln.reshape(1, A_DV), aq.reshape(DB, 1, A_QK_W), akf.reshape(DB, 1, A_QK_W),
      avf.reshape(DB, 1, A_V_W), *([ck] * npages), *([cv] * npages))
    return out.reshape(DB, A_V_W)


def kernel(x_prompt, x_sample, state_ret, cache_k, cache_v, page_table, g_mix, w_in, lam_q1, lam_k1, lam_q2, lam_k2,
           g_subln, w_br_ret, w_br_att, w_out, g_ffn, w_query, sub_keys, expert_u, expert_v, g_final):
    B, S, D = x_prompt.shape
    DB, L = x_sample.shape[:2]
    depth = w_in.shape[0]
    assert L == 1 and D == D_MODEL, "the sample kernels handle one new token per sample"
    past_len = page_table.shape[1] * cache_k.shape[2]
    tab_p = _rotation_tables(jnp.arange(S, dtype=F32))
    tab_s = _rotation_tables(jnp.tile(past_len + jnp.arange(L, dtype=jnp.int32), DB).astype(F32))
    hp = x_prompt.reshape(B * S, D)
    hs = x_sample.reshape(DB * L, D)
    rs_p, k_p, v_p, rs_s, k_s, v_s = [], [], [], [], [], []
    for l in range(depth):
        lam_init = 0.8 - 0.6 * math.exp(-0.3 * l)
        last = l == depth - 1
        w_in_bf = w_in[l].astype(BF16)
        wr, wa, wo, wq = (w[l].astype(BF16) for w in (w_br_ret, w_br_att, w_out, w_query))
        sk = sub_keys[l].reshape(2 * P_HEADS, P_NKEYS, P_HALF).astype(BF16)
        u_bf = expert_u[l].astype(BF16)
        vt_bf = expert_v[l].T.astype(BF16)
        lam4 = jnp.stack([lam_q1[l], lam_k1[l], lam_q2[l], lam_k2[l]]).astype(F32)

        def tail(x2d, r_out, a_out, ga, gb, tm, tg, tme):
            h, z2, sc_t = _merge(x2d, r_out, a_out, ga, gb, wr, wa, wo, g_ffn[l], wq, sk, tm)
            rn, ra, rk, rb = _route(sc_t, tg)
            return _experts(z2, h, u_bf, vt_bf, rn, ra, rk, rb, g_final, last, tme, 512)

        rq, rk_, rv, aq, akb, akf, avb, avf, ga, gb = _inproj(hp, g_mix[l], w_in_bf, tab_p, 256)
        r_out, st = _ret_prompt(rq, rk_, rv, B, S, 256)
        a_out = _attn_prompt(aq, akb, avb, lam4, g_subln[l], lam_init, B, S, 512)
        hp = tail(hp, r_out, a_out, ga, gb, 256, 128, 512)
        rs_p.append(st)
        k_p.append(akf.reshape(B, S, A_HEADS, 2 * A_DQ))
        v_p.append(avf.reshape(B, S, A_HEADS, A_DV))

        rq, rk_, rv, aq, akb, akf, avb, avf, ga, gb = _inproj(hs, g_mix[l], w_in_bf, tab_s, DB * L)
        r_out, st = _ret_sample(rq, rk_, rv, state_ret[l], 8)
        a_out = _attn_sample(aq, akf, avf, cache_k[l], cache_v[l], page_table, lam4, g_subln[l], lam_init)
        hs = tail(hs, r_out, a_out, ga, gb, DB * L, DB * L, DB * L)
        rs_s.append(st)
        k_s.append(akf.reshape(DB, L, A_HEADS, 2 * A_DQ))
        v_s.append(avf.reshape(DB, L, A_HEADS, A_DV))
    return (hp.reshape(B, S, D), hs.reshape(DB, L, D), jnp.stack(rs_p), jnp.stack(k_p), jnp.stack(v_p),
            jnp.stack(rs_s), jnp.stack(k_s), jnp.stack(v_s))
```

```python
import functools
import math

import jax
import jax.numpy as jnp
import numpy as np
from jax import lax
from jax.experimental import pallas as pl
from jax.experimental.pallas import tpu as pltpu

F32 = jnp.float32
BF16 = jnp.bfloat16

D_MODEL = 1024
R_HEADS, R_DK, R_DV = 4, 128, 256
R_ROT_BASE = 10000.0
A_HEADS, A_DQ, A_DV = 4, 64, 128
ROPE_THETA = 500000.0
ROT_DIM = A_DQ // 4
R_QK_W = R_HEADS * R_DK
R_V_W = R_HEADS * R_DV
A_QK_W = A_HEADS * 2 * A_DQ
A_V_W = A_HEADS * A_DV
IN_SIZES = (R_QK_W, R_QK_W, R_V_W, A_QK_W, A_QK_W, A_V_W, D_MODEL, D_MODEL)
IN_OFFS = tuple(int(v) for v in np.cumsum((0,) + IN_SIZES))
NORM_EPS = 1e-6
SUBLN_EPS = 1e-5
LANES = 128
V7X_VMEM_LIMIT = 56 * 1024 * 1024


def _cparams(sem, flags=None):
    return pltpu.CompilerParams(dimension_semantics=sem, vmem_limit_bytes=V7X_VMEM_LIMIT, flags=flags)


def _const_spec(shape):
    nd = len(shape)
    return pl.BlockSpec(shape, lambda *_: (0,) * nd)


def _rotation_tables(posf):
    lane = jnp.arange(LANES)
    angle = 1.0 / (R_ROT_BASE ** jnp.linspace(0.0, 1.0, R_DK // 2, dtype=F32))
    ang = posf[:, None] * angle[None, :]
    cos_r = jnp.repeat(jnp.cos(ang), 2, axis=1)
    sin_r = jnp.repeat(jnp.sin(ang), 2, axis=1)
    even = (lane % 2 == 0)[None, :]
    sr_even = jnp.where(even, -sin_r, 0.0)
    sr_odd = jnp.where(even, 0.0, sin_r)
    half = ROT_DIM // 2
    inv = ROPE_THETA ** (-jnp.arange(0, ROT_DIM, 2, dtype=F32) / ROT_DIM)
    ang_a = posf[:, None] * inv[None, :]
    within = lane % A_DQ
    idx = within % half
    cos_a = jnp.where((within < ROT_DIM)[None, :], jnp.cos(ang_a)[:, idx], 1.0)
    sin_a = jnp.sin(ang_a)[:, idx]
    sa_lo = jnp.where((within < half)[None, :], -sin_a, 0.0)
    sa_hi = jnp.where(((within >= half) & (within < ROT_DIM))[None, :], sin_a, 0.0)
    return tuple(t.astype(F32) for t in (cos_r, sr_even, sr_odd, cos_a, sa_lo, sa_hi))


def _inproj_kernel(x_ref, g_ref, w_ref, cr_ref, se_ref, so_ref, ca_ref, sl_ref, sh_ref,
                   rq_ref, rk_ref, rv_ref, aq_ref, akb_ref, akf_ref, avb_ref, avf_ref, ga_ref, gb_ref):
    x = x_ref[...]
    ms = jnp.mean(x * x, axis=-1, keepdims=True)
    z = (x * lax.rsqrt(ms + NORM_EPS) * g_ref[...]).astype(BF16)

    def proj(k):
        return jnp.dot(z, w_ref[:, IN_OFFS[k]:IN_OFFS[k + 1]], preferred_element_type=F32)

    cr, se, so = cr_ref[...], se_ref[...], so_ref[...]
    ca, sl, sh = ca_ref[...], sl_ref[...], sh_ref[...]

    def ret_rot(t):
        return t * cr + pltpu.roll(t, LANES - 1, 1) * se + pltpu.roll(t, 1, 1) * so

    def att_rot(t):
        h = ROT_DIM // 2
        return t * ca + pltpu.roll(t, LANES - h, 1) * sl + pltpu.roll(t, h, 1) * sh

    rq = proj(0)
    rk = proj(1)
    for hh in range(R_HEADS):
        s = slice(hh * LANES, (hh + 1) * LANES)
        rq_ref[:, s] = ret_rot(rq[:, s]).astype(BF16)
        rk_ref[:, s] = (ret_rot(rk[:, s]) * (R_DK ** -0.5)).astype(BF16)
    rv_ref[...] = proj(2).astype(BF16)
    aq = proj(3)
    ak = proj(4)
    for hh in range(A_HEADS):
        s = slice(hh * LANES, (hh + 1) * LANES)
        aq_ref[:, s] = (att_rot(aq[:, s]) * (A_DQ ** -0.5)).astype(BF16)
        akr = att_rot(ak[:, s])
        akf_ref[:, s] = akr
        akb_ref[:, s] = akr.astype(BF16)
    av = proj(5)
    avf_ref[...] = av
    avb_ref[...] = av.astype(BF16)
    ga_ref[...] = proj(6)
    gb_ref[...] = proj(7)


def _inproj(x2d, g_mix, w_in_bf, tables, tm):
    T = x2d.shape[0]
    P = tables[0].shape[0]
    nper = P // tm
    row = lambda i: (i, 0)
    tab = pl.BlockSpec((tm, LANES), lambda i: (i % nper, 0))
    widths_dtypes = ((R_QK_W, BF16), (R_QK_W, BF16), (R_V_W, BF16), (A_QK_W, BF16), (A_QK_W, BF16), (A_QK_W, F32),
                     (A_V_W, BF16), (A_V_W, F32), (D_MODEL, F32), (D_MODEL, F32))
    return pl.pallas_call(
        _inproj_kernel,
        grid=(T // tm,),
        in_specs=[pl.BlockSpec((tm, D_MODEL), row), _const_spec((1, D_MODEL)), _const_spec(w_in_bf.shape)] + [tab] * 6,
        out_specs=[pl.BlockSpec((tm, w), row) for w, _ in widths_dtypes],
        out_shape=[jax.ShapeDtypeStruct((T, w), dt) for w, dt in widths_dtypes],
        compiler_params=_cparams(("parallel",)),
        name="inproj",
    )(x2d, g_mix.reshape(1, D_MODEL), w_in_bf, *tables)


def _log_decay(h):
    return float(np.log(np.float32(1.0) - np.float32(2.0) ** np.float32(-5.0 - h)))


def _rms(o, eps):
    return o * lax.rsqrt(jnp.mean(o * o, axis=-1, keepdims=True) + eps)


def _ret_prompt_kernel(q_ref, k_ref, v_ref, o_ref, st_ref):
    C = q_ref.shape[0]

    @pl.when(pl.program_id(1) == 0)
    def _():
        st_ref[...] = jnp.zeros_like(st_ref)

    li = lax.broadcasted_iota(jnp.int32, (C, C), 0)
    mi = lax.broadcasted_iota(jnp.int32, (C, C), 1)
    rel = (li - mi).astype(F32)
    row = lax.broadcasted_iota(jnp.int32, (C, 1), 0).astype(F32)
    for h in range(R_HEADS):
        lg = _log_decay(h)
        dmask = jnp.where(rel >= 0.0, jnp.exp(lg * jnp.maximum(rel, 0.0)), 0.0)
        q = q_ref[:, h * R_DK:(h + 1) * R_DK]
        k = k_ref[:, h * R_DK:(h + 1) * R_DK]
        v = v_ref[:, h * R_DV:(h + 1) * R_DV]
        st = st_ref[0, h]
        s = lax.dot_general(q, k, (((1,), (1,)), ((), ())), preferred_element_type=F32)
        intra = jnp.dot((s * dmask).astype(BF16), v, preferred_element_type=F32)
        inter = jnp.dot(q, st.astype(BF16), preferred_element_type=F32) * jnp.exp(lg * (row + 1.0))
        o_ref[:, h * R_DV:(h + 1) * R_DV] = _rms(intra + inter, NORM_EPS).astype(BF16)
        kd = (k.astype(F32) * jnp.exp(lg * (C - 1.0 - row))).astype(BF16)
        upd = lax.dot_general(kd, v, (((0,), (0,)), ((), ())), preferred_element_type=F32)
        st_ref[0, h] = st * math.exp(lg * C) + upd


def _ret_prompt(rq, rk, rv, B, S, C):
    n = S // C
    row = lambda b, c: (b * n + c, 0)
    return pl.pallas_call(
        _ret_prompt_kernel,
        grid=(B, n),
        in_specs=[pl.BlockSpec((C, R_QK_W), row), pl.BlockSpec((C, R_QK_W), row), pl.BlockSpec((C, R_V_W), row)],
        out_specs=[pl.BlockSpec((C, R_V_W), row), pl.BlockSpec((1, R_HEADS, R_DK, R_DV), lambda b, c: (b, 0, 0, 0))],
        out_shape=[jax.ShapeDtypeStruct((B * S, R_V_W), BF16), jax.ShapeDtypeStruct((B, R_HEADS, R_DK, R_DV), F32)],
        compiler_params=_cparams(("parallel", "arbitrary")),
        name="ret_prompt",
    )(rq, rk, rv)


def _lambda(lam_ref, lam_init):
    lam = lam_ref[...]
    l1 = jnp.sum(lam[0:1] * lam[1:2], axis=-1, keepdims=True)
    l2 = jnp.sum(lam[2:3] * lam[3:4], axis=-1, keepdims=True)
    return jnp.exp(l1) - jnp.exp(l2) + lam_init


def _attn_prompt_kernel(lam_ref, gs_ref, q_ref, k_ref, v_ref, o_ref, q2_ref, m_ref, l_ref, acc_ref, *, lam_init):
    qi = pl.program_id(2)
    ki = pl.program_id(3)
    tq = q_ref.shape[0]
    tk = k_ref.shape[0]

    @pl.when(ki == 0)
    def _():
        q = q_ref[...]
        lane = lax.broadcasted_iota(jnp.int32, q.shape, 1)
        zero = jnp.zeros_like(q)
        q2_ref[0:tq] = jnp.where(lane < A_DQ, q, zero)
        q2_ref[tq:2 * tq] = jnp.where(lane >= A_DQ, q, zero)
        m_ref[...] = jnp.full_like(m_ref, -jnp.inf)
        l_ref[...] = jnp.zeros_like(l_ref)
        acc_ref[...] = jnp.zeros_like(acc_ref)

    def step(masked):
        s = lax.dot_general(q2_ref[...], k_ref[...], (((1,), (1,)), ((), ())), preferred_element_type=F32)
        if masked:
            r = lax.broadcasted_iota(jnp.int32, (tq, tk), 0)
            c = lax.broadcasted_iota(jnp.int32, (tq, tk), 1)
            keep = c <= r
            keep = jnp.concatenate([keep, keep], axis=0)
            s = jnp.where(keep, s, -1e30)
        m_prev = m_ref[...]
        m_new = jnp.maximum(m_prev, jnp.max(s, axis=-1, keepdims=True))
        alpha = jnp.exp(m_prev - m_new)
        p = jnp.exp(s - pltpu.repeat(m_new, tk // LANES, axis=1))
        l_ref[...] = alpha * l_ref[...] + jnp.sum(p, axis=-1, keepdims=True)
        acc_ref[...] = alpha * acc_ref[...] + jnp.dot(p.astype(BF16), v_ref[...], preferred_element_type=F32)
        m_ref[...] = m_new

    @pl.when(ki < qi)
    def _():
        step(False)

    @pl.when(ki == qi)
    def _():
        step(True)
        lam = _lambda(lam_ref, lam_init)
        o = acc_ref[0:tq] / l_ref[0:tq] - lam * (acc_ref[tq:2 * tq] / l_ref[tq:2 * tq])
        o_ref[...] = (_rms(o, SUBLN_EPS) * gs_ref[...] * (1.0 - lam_init)).astype(BF16)


def _attn_prompt(aq, ak, av, lam4, g_subln, lam_init, B, S, t):
    n = S // t
    return pl.pallas_call(
        functools.partial(_attn_prompt_kernel, lam_init=lam_init),
        grid=(B, A_HEADS, n, n),
        in_specs=[_const_spec((4, A_DQ)), _const_spec((1, A_DV)),
                  pl.BlockSpec((t, LANES), lambda b, h, qi, ki: (b * n + qi, h)),
                  pl.BlockSpec((t, LANES), lambda b, h, qi, ki: (b * n + jnp.minimum(ki, qi), h)),
                  pl.BlockSpec((t, LANES), lambda b, h, qi, ki: (b * n + jnp.minimum(ki, qi), h))],
        out_specs=pl.BlockSpec((t, LANES), lambda b, h, qi, ki: (b * n + qi, h)),
        out_shape=jax.ShapeDtypeStruct((B * S, A_V_W), BF16),
        scratch_shapes=[pltpu.VMEM((2 * t, LANES), BF16), pltpu.VMEM((2 * t, LANES), F32),
                        pltpu.VMEM((2 * t, LANES), F32), pltpu.VMEM((2 * t, LANES), F32)],
        compiler_params=_cparams(("parallel", "parallel", "parallel", "arbitrary")),
        name="attn_prompt",
    )(lam4, g_subln.reshape(1, A_DV), aq, ak, av)


P_HEADS, P_NKEYS, P_DKEY, P_TOPK = 8, 128, 256, 16
P_HALF = P_DKEY // 2
P_QW = P_HEADS * P_DKEY


def _merge_kernel(x_ref, r_ref, a_ref, ga_ref, gb_ref, wr_ref, wa_ref, wo_ref, gf_ref, wq_ref, sk_ref,
                  h_ref, z2_ref, sc_ref):
    mr = jnp.dot(r_ref[...], wr_ref[...], preferred_element_type=F32)
    ma = jnp.dot(a_ref[...], wa_ref[...], preferred_element_type=F32)
    m = jax.nn.sigmoid(ga_ref[...]) * mr + jax.nn.sigmoid(gb_ref[...]) * ma
    h = x_ref[...] + jnp.dot(m.astype(BF16), wo_ref[...], preferred_element_type=F32)
    h_ref[...] = h
    z2 = (_rms(h, NORM_EPS) * gf_ref[...]).astype(BF16)
    z2_ref[...] = z2
    qry = jnp.dot(z2, wq_ref[...], preferred_element_type=F32).astype(BF16)
    for hc in range(2 * P_HEADS):
        s = slice(hc * P_HALF, (hc + 1) * P_HALF)
        sc_ref[hc * P_NKEYS:(hc + 1) * P_NKEYS, :] = lax.dot_general(
            sk_ref[hc], qry[:, s], (((1,), (1,)), ((), ())), preferred_element_type=F32)


def _merge(x2d, r_out, a_out, ga, gb, wr, wa, wo, g_ffn, wq, sk, tm):
    T = x2d.shape[0]
    row = lambda i: (i, 0)
    nsc = 2 * P_HEADS * P_NKEYS
    return pl.pallas_call(
        _merge_kernel,
        grid=(T // tm,),
        in_specs=[pl.BlockSpec((tm, D_MODEL), row), pl.BlockSpec((tm, R_V_W), row), pl.BlockSpec((tm, A_V_W), row),
                  pl.BlockSpec((tm, D_MODEL), row), pl.BlockSpec((tm, D_MODEL), row),
                  _const_spec(wr.shape), _const_spec(wa.shape), _const_spec(wo.shape), _const_spec((1, D_MODEL)),
                  _const_spec(wq.shape), _const_spec(sk.shape)],
        out_specs=[pl.BlockSpec((tm, D_MODEL), row), pl.BlockSpec((tm, D_MODEL), row),
                   pl.BlockSpec((nsc, tm), lambda i: (0, i))],
        out_shape=[jax.ShapeDtypeStruct((T, D_MODEL), F32), jax.ShapeDtypeStruct((T, D_MODEL), BF16),
                   jax.ShapeDtypeStruct((nsc, T), F32)],
        compiler_params=_cparams(("parallel",)),
        name="merge",
    )(x2d, r_out, a_out, ga, gb, wr, wa, wo, g_ffn.reshape(1, D_MODEL), wq, sk)


_CAND_LIMITS = tuple(P_TOPK // (r + 1) for r in range(P_TOPK))
_CAND_ROWS = 80


def _extract_top(vals, order, rounds, exact):
    cur = vals
    taken = jnp.full(vals.shape, float(rounds), F32)
    big = float(2 ** 20)
    tops = []
    for r in range(rounds):
        m = jnp.max(cur, axis=0, keepdims=True)
        if exact:
            first = jnp.min(jnp.where(cur == m, order, big), axis=0, keepdims=True)
            sel = order == first
        else:
            sel = cur == m
        cur = jnp.where(sel, -jnp.inf, cur)
        taken = jnp.where(sel, float(r), taken)
        tops.append(m)
    n_taken = jnp.sum(jnp.where(taken < float(rounds), 1.0, 0.0), axis=0, keepdims=True)
    return tops, taken, jnp.where(n_taken != float(rounds), 1.0, 0.0)


def _route_kernel(sc_ref, n_ref, a_ref, k_ref, b_ref):
    t = sc_ref.shape[1]
    key_order = lax.broadcasted_iota(jnp.int32, (P_NKEYS, t), 0).astype(F32)
    crow = lax.broadcasted_iota(jnp.int32, (_CAND_ROWS, t), 0)
    cand_r = jnp.where(crow < 16, 0, jnp.where(crow < 72, (crow - 8) // 8, crow - 64))
    cand_c = jnp.where(crow < 16, crow, jnp.where(crow < 72, crow % 8, 0))
    limit = jnp.zeros_like(crow)
    for r in range(P_TOPK):
        limit = jnp.where(cand_r == r, _CAND_LIMITS[r], limit)
    cand_valid = cand_c < limit
    cand_pos = (cand_r * P_TOPK + cand_c).astype(F32)

    def head(h, tied, exact):
        s0 = sc_ref[pl.ds(pl.multiple_of(h * 2 * P_NKEYS, P_NKEYS), P_NKEYS), :]
        s1 = sc_ref[pl.ds(pl.multiple_of(h * 2 * P_NKEYS + P_NKEYS, P_NKEYS), P_NKEYS), :]
        top0, rank0, tied0 = _extract_top(s0, key_order, P_TOPK, exact)
        top1, rank1, tied1 = _extract_top(s1, key_order, P_TOPK, exact)
        s1_lo = jnp.concatenate(top1[0:8], axis=0)
        s1_hi = jnp.concatenate(top1[8:16], axis=0)
        s0_hi = jnp.concatenate(top0[8:16], axis=0)
        cand = jnp.concatenate([top0[0] + s1_lo, top0[0] + s1_hi] + [top0[r] + s1_lo for r in range(1, 8)]
                               + [s0_hi + top1[0]], axis=0)
        cand = jnp.where(cand_valid, cand, -jnp.inf)
        ctop, ctaken, tiedc = _extract_top(cand, cand_pos, P_TOPK, exact)
        chosen = ctaken < float(P_TOPK)
        e = jnp.where(chosen, jnp.exp(cand - ctop[0]), 0.0)
        z_inv = 1.0 / jnp.sum(e, axis=0, keepdims=True)
        cnt = chosen.astype(F32)
        n_rows = [jnp.sum(cnt[0:16], axis=0, keepdims=True)]
        n_rows += [jnp.sum(cnt[16 + 8 * (r - 1):24 + 8 * (r - 1)], axis=0, keepdims=True) for r in range(1, 8)]
        n_rows += [cnt[72 + r:73 + r] for r in range(8)]
        n_dense = jnp.zeros((P_NKEYS, t), F32)
        for r in range(P_TOPK):
            n_dense = jnp.where(rank0 == float(r), n_rows[r], n_dense)
        o = pl.ds(pl.multiple_of(h * P_NKEYS, P_NKEYS), P_NKEYS)
        n_ref[o, :] = n_dense
        a_ref[o, :] = jnp.exp(s0 - top0[0]) * z_inv
        k_ref[o, :] = rank1
        b_ref[o, :] = jnp.exp(s1 - top1[0])
        return jnp.maximum(tied, jnp.maximum(jnp.maximum(tied0, tied1), tiedc))

    tied = lax.fori_loop(0, P_HEADS, functools.partial(head, exact=False), jnp.zeros((1, t), F32))

    @pl.when(jnp.max(tied) > 0.0)
    def _():
        lax.fori_loop(0, P_HEADS, functools.partial(head, exact=True), jnp.zeros((1, t), F32))


def _route(sc_t, tg):
    T = sc_t.shape[1]
    col = lambda i: (0, i)
    return pl.pallas_call(
        _route_kernel,
        grid=(T // tg,),
        in_specs=[pl.BlockSpec((2 * P_HEADS * P_NKEYS, tg), col)],
        out_specs=[pl.BlockSpec((P_HEADS * P_NKEYS, tg), col)] * 4,
        out_shape=[jax.ShapeDtypeStruct((P_HEADS * P_NKEYS, T), F32)] * 4,
        compiler_params=_cparams(("parallel",)),
        name="route",
    )(sc_t)


def _expert_kernel(z_ref, h_ref, u_ref, vt_ref, n_ref, a_ref, k_ref, b_ref, gfin_ref, y_ref, acc_ref, hid_ref, act_ref,
                   kb_ref, bb_ref, *, final_norm):
    e = pl.program_id(1)
    te = u_ref.shape[0]
    per = te // P_NKEYS

    @pl.when(e == 0)
    def _():
        acc_ref[...] = jnp.zeros_like(acc_ref)
        kb_ref[...] = k_ref[...].astype(BF16)
        bb_ref[...] = b_ref[...].astype(BF16)

    tm = z_ref.shape[0]
    zero = jnp.zeros((P_NKEYS, LANES), BF16)
    hid_ref[...] = lax.dot_general(u_ref[...], z_ref[...], (((1,), (1,)), ((), ())),
                                   preferred_element_type=F32)
    for ii in range(per):
        rows = slice(ii * P_NKEYS, (ii + 1) * P_NKEYS)
        n_rows = [n_ref[pl.ds(h * P_NKEYS + e * per + ii, 1), :].astype(BF16) for h in range(P_HEADS)]
        a_rows = [a_ref[pl.ds(h * P_NKEYS + e * per + ii, 1), :].astype(BF16) for h in range(P_HEADS)]
        for tb in range(tm // LANES):
            cols = slice(tb * LANES, (tb + 1) * LANES)
            gate = zero
            for h in range(P_HEADS):
                kj = kb_ref[h * P_NKEYS:(h + 1) * P_NKEYS, cols]
                bj = bb_ref[h * P_NKEYS:(h + 1) * P_NKEYS, cols]
                gate = gate + a_rows[h][:, cols] * jnp.where(kj < n_rows[h][:, cols], bj, zero)
            hid = hid_ref[rows, cols]
            gelu = 0.5 * hid * (1.0 + lax.erf(hid * math.sqrt(0.5)))
            act_ref[rows, cols] = gelu.astype(BF16) * gate
    acc_ref[...] += jnp.dot(vt_ref[...], act_ref[...], preferred_element_type=F32)

    @pl.when(e == pl.num_programs(1) - 1)
    def _():
        hf = h_ref[...] + acc_ref[...].T
        y_ref[...] = _rms(hf, NORM_EPS) * gfin_ref[...] if final_norm else hf


def _experts(z2, h, u_bf, vt_bf, rn, ra, rk, rb, g_final, final_norm, tm, te):
    T = z2.shape[0]
    E = u_bf.shape[0]
    row = lambda i, e: (i, 0)
    rcol = lambda i, e: (0, i)
    rspec = pl.BlockSpec((P_HEADS * P_NKEYS, tm), rcol)
    route_bf16 = pltpu.VMEM((P_HEADS * P_NKEYS, tm), BF16)
    return pl.pallas_call(
        functools.partial(_expert_kernel, final_norm=final_norm),
        grid=(T // tm, E // te),
        in_specs=[pl.BlockSpec((tm, D_MODEL), row), pl.BlockSpec((tm, D_MODEL), row),
                  pl.BlockSpec((te, D_MODEL), lambda i, e: (e, 0)), pl.BlockSpec((D_MODEL, te), lambda i, e: (0, e)),
                  rspec, rspec, rspec, rspec, _const_spec((1, D_MODEL))],
        out_specs=pl.BlockSpec((tm, D_MODEL), row),
        out_shape=jax.ShapeDtypeStruct((T, D_MODEL), F32),
        scratch_shapes=[pltpu.VMEM((D_MODEL, tm), F32), pltpu.VMEM((te, tm), F32), pltpu.VMEM((te, tm), BF16),
                        route_bf16, route_bf16],
        compiler_params=_cparams(("parallel", "arbitrary")),
        name="experts",
    )(z2, h, u_bf, vt_bf, rn, ra, rk, rb, g_final.reshape(1, D_MODEL))


def _ret_sample_kernel(q_ref, k_ref, v_ref, st_ref, o_ref, ns_ref):
    nb = q_ref.shape[0]
    q = q_ref[...].astype(F32)
    k = k_ref[...].astype(F32)
    v = v_ref[...].astype(F32)
    for h in range(R_HEADS):
        gamma = math.exp(_log_decay(h))
        qh = q[:, h * R_DK:(h + 1) * R_DK]
        kh = k[:, h * R_DK:(h + 1) * R_DK]
        vh = v[:, h * R_DV:(h + 1) * R_DV]
        q_t = qh.T
        k_t = kh.T
        qk = jnp.sum(qh * kh, axis=-1, keepdims=True)
        rows = []
        for b in range(nb):
            st = st_ref[b, h]
            vrow = vh[b:b + 1]
            inter = jnp.sum(q_t[:, b:b + 1] * st, axis=0, keepdims=True) * gamma
            rows.append(qk[b:b + 1] * vrow + inter)
            ns_ref[b, h] = st * gamma + k_t[:, b:b + 1] * vrow
        o = jnp.concatenate(rows, axis=0)
        o_ref[:, h * R_DV:(h + 1) * R_DV] = _rms(o, NORM_EPS).astype(BF16)


def _ret_sample(rq, rk, rv, state, nb):
    DB = rq.shape[0]
    row = lambda i: (i, 0)
    st_spec = pl.BlockSpec((nb, R_HEADS, R_DK, R_DV), lambda i: (i, 0, 0, 0))
    return pl.pallas_call(
        _ret_sample_kernel,
        grid=(DB // nb,),
        in_specs=[pl.BlockSpec((nb, R_QK_W), row), pl.BlockSpec((nb, R_QK_W), row), pl.BlockSpec((nb, R_V_W), row), st_spec],
        out_specs=[pl.BlockSpec((nb, R_V_W), row), st_spec],
        out_shape=[jax.ShapeDtypeStruct((DB, R_V_W), BF16), jax.ShapeDtypeStruct(state.shape, F32)],
        compiler_params=_cparams(("parallel",)),
        name="ret_sample",
    )(rq, rk, rv, state)


def _attn_sample_kernel(pt_ref, lam_ref, gs_ref, q_ref, kn_ref, vn_ref, *rest, npages, lam_init):
    kp = rest[:npages]
    vp = rest[npages:2 * npages]
    o_ref = rest[2 * npages]
    nmap = 2 * A_HEADS
    q = q_ref[0].astype(F32)
    hs = [slice(h * LANES, (h + 1) * LANES) for h in range(A_HEADS)]
    sub = lax.broadcasted_iota(jnp.int32, (nmap, LANES), 0)
    lane = lax.broadcasted_iota(jnp.int32, (nmap, LANES), 1)
    qrows = jnp.concatenate([q[:, hs[r // 2]] for r in range(nmap)], axis=0)
    qblk = jnp.where(lane // A_DQ == sub % 2, qrows, 0.0).astype(BF16)
    pad = jnp.zeros((nmap - A_HEADS, LANES), F32)
    k_new = jnp.concatenate([kn_ref[0][:, hs[h]] for h in range(A_HEADS)] + [pad], axis=0).astype(BF16)
    v_new = jnp.concatenate([vn_ref[0][:, hs[h]] for h in range(A_HEADS)] + [pad], axis=0).astype(BF16)
    nt = (((1,), (1,)), ((), ()))
    tn = (((0,), (0,)), ((), ()))

    def masked_scores(keys, own):
        n = keys.shape[0]
        r = lax.broadcasted_iota(jnp.int32, (n, nmap), 0)
        c = lax.broadcasted_iota(jnp.int32, (n, nmap), 1)
        ok = (r == c // 2) if own else (r % A_HEADS == c // 2)
        return jnp.where(ok, lax.dot_general(keys, qblk, nt, preferred_element_type=F32), -jnp.inf)

    scores = [masked_scores(kp[p][...].astype(BF16), False) for p in range(npages)]
    s_new = masked_scores(k_new, True)
    m = jnp.max(s_new, axis=0, keepdims=True)
    for s in scores:
        m = jnp.maximum(m, jnp.max(s, axis=0, keepdims=True))
    p_new = jnp.exp(s_new - m)
    probs = [jnp.exp(s - m) for s in scores]
    l = jnp.sum(p_new, axis=0, keepdims=True)
    for p in probs:
        l = l + jnp.sum(p, axis=0, keepdims=True)
    inv_l = 1.0 / l
    acc = lax.dot_general((p_new * inv_l).astype(BF16), v_new, tn, preferred_element_type=F32)
    for p in range(npages):
        acc = acc + lax.dot_general((probs[p] * inv_l).astype(BF16), vp[p][...].astype(BF16), tn,
                                    preferred_element_type=F32)
    lam = _lambda(lam_ref, lam_init)
    outs = []
    for h in range(A_HEADS):
        o = acc[2 * h:2 * h + 1] - lam * acc[2 * h + 1:2 * h + 2]
        outs.append(_rms(o, SUBLN_EPS) * gs_ref[...] * (1.0 - lam_init))
    o_ref[0] = jnp.concatenate(outs, axis=1).astype(BF16)


def _attn_sample(aq, akf, avf, cache_k, cache_v, layer, page_table, lam4, g_subln, lam_init):
    DB = aq.shape[0]
    npages = page_table.shape[1]
    depth, npool, page = cache_k.shape[:3]
    cache_k = cache_k.reshape(depth, npool, page * A_HEADS, LANES)
    cache_v = cache_v.reshape(depth, npool, page * A_HEADS, LANES)
    tok = pl.BlockSpec((1, 1, A_QK_W), lambda b, pt: (b, 0, 0))
    page_specs = [pl.BlockSpec((None, None, page * A_HEADS, LANES),
                               functools.partial(lambda b, pt, p: (layer, pt[b, p], 0, 0), p=p))
                  for p in range(npages)]
    grid_spec = pltpu.PrefetchScalarGridSpec(
        num_scalar_prefetch=1, grid=(DB,),
        in_specs=[pl.BlockSpec((4, A_DQ), lambda b, pt: (0, 0)), pl.BlockSpec((1, A_DV), lambda b, pt: (0, 0)),
                  tok, tok, tok] + page_specs + page_specs,
        out_specs=tok)
    out = pl.pallas_call(
        functools.partial(_attn_sample_kernel, npages=npages, lam_init=lam_init),
        grid_spec=grid_spec,
        out_shape=jax.ShapeDtypeStruct((DB, 1, A_V_W), BF16),
        compiler_params=_cparams(("parallel",)),
        name="attn_sample",
    )(page_table, lam4, g_subln.reshape(1, A_DV), aq.reshape(DB, 1, A_QK_W), akf.reshape(DB, 1, A_QK_W),
      avf.reshape(DB, 1, A_V_W), *([cache_k] * npages), *([cache_v] * npages))
    return out.reshape(DB, A_V_W)


def kernel(x_prompt, x_sample, state_ret, cache_k, cache_v, page_table, g_mix, w_in, lam_q1, lam_k1, lam_q2, lam_k2,
           g_subln, w_br_ret, w_br_att, w_out, g_ffn, w_query, sub_keys, expert_u, expert_v, g_final):
    B, S, D = x_prompt.shape
    DB, L = x_sample.shape[:2]
    depth = w_in.shape[0]
    assert L == 1 and D == D_MODEL, "the sample kernels handle one new token per sample"
    past_len = page_table.shape[1] * cache_k.shape[2]
    tab_p = _rotation_tables(jnp.arange(S, dtype=F32))
    tab_s = _rotation_tables(jnp.tile(past_len + jnp.arange(L, dtype=jnp.int32), DB).astype(F32))
    hp = x_prompt.reshape(B * S, D)
    hs = x_sample.reshape(DB * L, D)
    rs_p, k_p, v_p, rs_s, k_s, v_s = [], [], [], [], [], []
    for l in range(depth):
        lam_init = 0.8 - 0.6 * math.exp(-0.3 * l)
        last = l == depth - 1
        w_in_bf = w_in[l].astype(BF16)
        wr, wa, wo, wq = (w[l].astype(BF16) for w in (w_br_ret, w_br_att, w_out, w_query))
        sk = sub_keys[l].reshape(2 * P_HEADS, P_NKEYS, P_HALF).astype(BF16)
        u_bf = expert_u[l].astype(BF16)
        vt_bf = expert_v[l].T.astype(BF16)
        lam4 = jnp.stack([lam_q1[l], lam_k1[l], lam_q2[l], lam_k2[l]]).astype(F32)

        def tail(x2d, r_out, a_out, ga, gb, tm, tg, tme):
            h, z2, sc_t = _merge(x2d, r_out, a_out, ga, gb, wr, wa, wo, g_ffn[l], wq, sk, tm)
            rn, ra, rk, rb = _route(sc_t, tg)
            return _experts(z2, h, u_bf, vt_bf, rn, ra, rk, rb, g_final, last, tme, 512)

        rq, rk_, rv, aq, akb, akf, avb, avf, ga, gb = _inproj(hp, g_mix[l], w_in_bf, tab_p, 256)
        r_out, st = _ret_prompt(rq, rk_, rv, B, S, 256)
        a_out = _attn_prompt(aq, akb, avb, lam4, g_subln[l], lam_init, B, S, 512)
        hp = tail(hp, r_out, a_out, ga, gb, 256, 128, 512)
        rs_p.append(st)
        k_p.append(akf.reshape(B, S, A_HEADS, 2 * A_DQ))
        v_p.append(avf.reshape(B, S, A_HEADS, A_DV))

        rq, rk_, rv, aq, akb, akf, avb, avf, ga, gb = _inproj(hs, g_mix[l], w_in_bf, tab_s, DB * L)
        r_out, st = _ret_sample(rq, rk_, rv, state_ret[l], 8)
        a_out = _attn_sample(aq, akf, avf, cache_k, cache_v, l, page_table, lam4, g_subln[l], lam_init)
        hs = tail(hs, r_out, a_out, ga, gb, DB * L, DB * L, DB * L)
        rs_s.append(st)
        k_s.append(akf.reshape(DB, L, A_HEADS, 2 * A_DQ))
        v_s.append(avf.reshape(DB, L, A_HEADS, A_DV))
    return (hp.reshape(B, S, D), hs.reshape(DB, L, D), jnp.stack(rs_p), jnp.stack(k_p), jnp.stack(v_p),
            jnp.stack(rs_s), jnp.stack(k_s), jnp.stack(v_s))
```
